```python
import math
import jax, jax.numpy as jnp
from jax import lax
import numpy as np

D_MODEL = 1024
BATCH = 4
SEQ = 8192
DEPTH = 1

CHUNK = 64
MIX_WIDTH = D_MODEL
A_HEADS = 4
A_HEAD_DIM = 128
A_WIDTH = A_HEADS * A_HEAD_DIM
SGU_BLOCK = 128
B_GROUPS = 4
B_GROUP_DIM = 128
B_WIDTH = B_GROUPS * B_GROUP_DIM
POOL_WINDOWS = (2, 4, 8, 16)
IN_WIDTH = 2 * A_WIDTH + B_WIDTH
D_FF = 2816
CONV_W = 3
N_MOD = 6
EPS = 1e-6

kernel_name = "hybrid_sgu_pool_convffn_block"


def rmsnorm(x, g):
    xf = x.astype(jnp.float32)
    y = xf * lax.rsqrt(jnp.mean(xf * xf, axis=-1, keepdims=True) + EPS)
    return (y * g.astype(jnp.float32)).astype(x.dtype)


def chunk_causal_block_mask():
    pos = jnp.arange(SGU_BLOCK)
    return (pos[None, :] // CHUNK) <= (pos[:, None] // CHUNK)


def sgu_mixer(a_in, norm_g, w_s, b_s):
    bsz, seq, _ = a_in.shape
    a = jax.nn.gelu(a_in)
    u, v = a[..., :A_WIDTH], a[..., A_WIDTH:]
    nb = seq // SGU_BLOCK
    v = v.reshape(bsz, nb, SGU_BLOCK, A_HEADS, A_HEAD_DIM)
    v = rmsnorm(v, norm_g[None, None, None])
    w_m = jnp.where(chunk_causal_block_mask()[None], w_s, jnp.zeros_like(w_s))
    z = jnp.einsum('hij,bnjhc->bnihc', w_m, v) + b_s.T[None, None, :, :, None]
    return u * z.reshape(bsz, seq, A_WIDTH)


def pool_mixer(p_in, w_pool, pool_scale):
    bsz, seq, _ = p_in.shape
    pg = p_in.reshape(bsz, seq, B_GROUPS, B_GROUP_DIM)
    t = jnp.arange(seq)
    outs = []
    for g, w in enumerate(POOL_WINDOWS):
        xf = pg[:, :, g].astype(jnp.float32)
        cs = jnp.cumsum(xf, axis=1)
        cs_full = jnp.pad(cs, ((0, 0), (w, 0), (0, 0)))
        win_sum = cs_full[:, w:] - cs_full[:, :seq]
        cnt = jnp.minimum(t + 1, w).astype(jnp.float32)[None, :, None]
        outs.append((win_sum / cnt - xf).astype(p_in.dtype))
    pooled = jnp.stack(outs, axis=2)
    y = jnp.einsum('bsgc,gcd->bsgd', pooled, w_pool)
    return y.reshape(bsz, seq, B_WIDTH) * pool_scale


def conv_ffn(h, w_up, conv_w, conv_b, w_down):
    up = jnp.einsum('bsd,df->bsf', h, w_up)
    upp = jnp.pad(up, ((0, 0), (CONV_W - 1, 0), (0, 0)))
    seq = up.shape[1]
    y = conv_b + upp[:, 0:seq] * conv_w[0] + upp[:, 1:seq + 1] * conv_w[1] + upp[:, 2:seq + 2] * conv_w[2]
    gate, val = y[..., :D_FF], y[..., D_FF:]
    return jnp.einsum('bsf,fd->bsd', jax.nn.silu(gate) * val, w_down)


def setup_inputs(seed: int = 0) -> dict:
    key = jax.random.key(seed)
    ks = jax.random.split(key, 20)
    f32 = jnp.float32
    nrm = lambda k, shape, s: jax.random.normal(k, shape, f32) * s
    L = DEPTH
    return {
        "x": nrm(ks[0], (BATCH, SEQ, D_MODEL), 1.0),
        "c": nrm(ks[1], (BATCH, D_MODEL), 1.0),
        "w_ada": nrm(ks[2], (L, D_MODEL, N_MOD * D_MODEL), D_MODEL ** -0.5),
        "b_ada": nrm(ks[3], (L, N_MOD * D_MODEL), 0.02),
        "pre_mix_g": 1.0 + nrm(ks[4], (L, D_MODEL), 0.02),
        "post_mix_g": 1.0 + nrm(ks[5], (L, D_MODEL), 0.02),
        "w_in": nrm(ks[6], (L, D_MODEL, IN_WIDTH), D_MODEL ** -0.5),
        "sgu_norm_g": 1.0 + nrm(ks[7], (L, A_HEADS, A_HEAD_DIM), 0.02),
        "w_spatial": nrm(ks[8], (L, A_HEADS, SGU_BLOCK, SGU_BLOCK), SGU_BLOCK ** -0.5),
        "b_spatial": 1.0 + nrm(ks[9], (L, A_HEADS, SGU_BLOCK), 0.02),
        "w_pool": nrm(ks[10], (L, B_GROUPS, B_GROUP_DIM, B_GROUP_DIM), B_GROUP_DIM ** -0.5),
        "pool_scale": 1.0 + nrm(ks[11], (L, B_WIDTH), 0.02),
        "w_out": nrm(ks[12], (L, MIX_WIDTH, D_MODEL), MIX_WIDTH ** -0.5),
        "pre_ffn_g": 1.0 + nrm(ks[13], (L, D_MODEL), 0.02),
        "post_ffn_g": 1.0 + nrm(ks[14], (L, D_MODEL), 0.02),
        "w_up": nrm(ks[15], (L, D_MODEL, 2 * D_FF), D_MODEL ** -0.5),
        "conv_w": nrm(ks[16], (L, CONV_W, 2 * D_FF), CONV_W ** -0.5),
        "conv_b": nrm(ks[17], (L, 2 * D_FF), 0.02),
        "w_down": nrm(ks[18], (L, D_FF, D_MODEL), D_FF ** -0.5),
    }


def reference(x, c, w_ada, b_ada, pre_mix_g, post_mix_g, w_in, sgu_norm_g, w_spatial,
              b_spatial, w_pool, pool_scale, w_out, pre_ffn_g, post_ffn_g, w_up, conv_w,
              conv_b, w_down):
    sc = jax.nn.silu(c)
    for l in range(DEPTH):
        mod = (jnp.einsum('bd,de->be', sc, w_ada[l]) + b_ada[l])[:, None, :]
        sh_m, sc_m, g_m, sh_f, sc_f, g_f = jnp.split(mod, N_MOD, axis=-1)

        h = rmsnorm(x, pre_mix_g[l]) * (1.0 + sc_m) + sh_m
        proj = jnp.einsum('bsd,de->bse', h, w_in[l])
        out_a = sgu_mixer(proj[..., :2 * A_WIDTH], sgu_norm_g[l], w_spatial[l], b_spatial[l])
        out_b = pool_mixer(proj[..., 2 * A_WIDTH:], w_pool[l], pool_scale[l])
        mixed = jnp.einsum('bse,ed->bsd', jnp.concatenate([out_a, out_b], axis=-1), w_out[l])
        x = x + g_m * rmsnorm(mixed, post_mix_g[l])

        h = rmsnorm(x, pre_ffn_g[l]) * (1.0 + sc_f) + sh_f
        f = conv_ffn(h, w_up[l], conv_w[l], conv_b[l], w_down[l])
        x = x + g_f * rmsnorm(f, post_ffn_g[l])
    return x
```

```python
import functools

import jax
import jax.numpy as jnp
from jax import lax
from jax.experimental import pallas as pl
from jax.experimental.pallas import tpu as pltpu

CHUNK = 64
A_HEADS = 4
A_HEAD_DIM = 128
A_WIDTH = A_HEADS * A_HEAD_DIM
SGU_BLOCK = 128
B_GROUPS = 4
B_GROUP_DIM = 128
B_WIDTH = B_GROUPS * B_GROUP_DIM
POOL_WINDOWS = (2, 4, 8, 16)
CONV_W = 3
N_MOD = 6
EPS = 1e-6

SUBLANES = 8
LANES = 128

POOL_HALO = 16
CONV_HALO = SUBLANES
SEQ_TILE_MIX = 256
SEQ_TILE_FFN = 256
FF_CHUNK = 256
MOD_COLS = 1536
VMEM_LIMIT_BYTES = 56 * 1024 * 1024


def _rms_scale(x):
    return x * lax.rsqrt(jnp.mean(x * x, axis=-1, keepdims=True) + EPS)


def _gelu_tanh(x):
    c = 0.7978845608028654
    return x * (0.5 * (1.0 + jnp.tanh(c * (x + 0.044715 * (x * x * x)))))


def _silu(x):
    return x * jax.nn.sigmoid(x)


def _bf16_dot(a, b):
    return jnp.dot(a.astype(jnp.bfloat16), b.astype(jnp.bfloat16), preferred_element_type=jnp.float32)


def _mod_kernel(c_ref, w_ref, b_ref, o_ref):
    sc = _silu(c_ref[...])
    o_ref[...] = jnp.dot(sc, w_ref[...], preferred_element_type=jnp.float32) + b_ref[...]


def _modulation(c, w_ada, b_ada):
    bsz, d = c.shape
    n = w_ada.shape[1]
    rows = -(-bsz // SUBLANES) * SUBLANES
    c_pad = jnp.pad(c, ((0, rows - bsz), (0, 0)))
    out = pl.pallas_call(
        _mod_kernel,
        grid=(n // MOD_COLS,),
        in_specs=[
            pl.BlockSpec((rows, d), lambda j: (0, 0)),
            pl.BlockSpec((d, MOD_COLS), lambda j: (0, j)),
            pl.BlockSpec((1, MOD_COLS), lambda j: (0, j)),
        ],
        out_specs=pl.BlockSpec((rows, MOD_COLS), lambda j: (0, j)),
        out_shape=jax.ShapeDtypeStruct((rows, n), jnp.float32),
        compiler_params=pltpu.CompilerParams(dimension_semantics=("arbitrary",),
                                             vmem_limit_bytes=VMEM_LIMIT_BYTES),
        name="adaln_modulation",
    )(c_pad, w_ada, b_ada.reshape(1, n))
    return out[:bsz].reshape(bsz, N_MOD, d)


def _mix_kernel(x_ref, mod_ref, pre_g_ref, post_g_ref, w_in_ref, sgu_g_ref, w_sp_ref, b_sp_ref,
                w_pool_ref, pool_scale_ref, w_out_ref, o_ref, pbuf_ref):
    s = pl.program_id(1)
    tile = x_ref.shape[0]
    n_blocks = tile // SGU_BLOCK

    x = x_ref[...]
    mod = mod_ref[...]
    sh_m, sc_m, g_m = mod[0:1], mod[1:2], mod[2:3]

    h = _rms_scale(x) * pre_g_ref[...] * (1.0 + sc_m) + sh_m
    proj = _bf16_dot(h, w_in_ref[...])

    a = _gelu_tanh(proj[:, :2 * A_WIDTH])
    u, v = a[:, :A_WIDTH], a[:, A_WIDTH:]
    row = lax.broadcasted_iota(jnp.int32, (SGU_BLOCK, SGU_BLOCK), 0)
    col = lax.broadcasted_iota(jnp.int32, (SGU_BLOCK, SGU_BLOCK), 1)
    causal = (col // CHUNK) <= (row // CHUNK)
    sgu_g = sgu_g_ref[...]
    b_sp = b_sp_ref[...]
    out_a_heads = []
    for hd in range(A_HEADS):
        cols = slice(hd * A_HEAD_DIM, (hd + 1) * A_HEAD_DIM)
        vn = (_rms_scale(v[:, cols]) * sgu_g[:, cols]).astype(jnp.bfloat16)
        v_cat = jnp.concatenate(
            [vn[n * SGU_BLOCK:(n + 1) * SGU_BLOCK] for n in range(n_blocks)], axis=1)
        w_m = jnp.where(causal, w_sp_ref[hd], 0.0)
        z = _bf16_dot(w_m, v_cat) + b_sp[:, hd:hd + 1]
        z_rows = jnp.concatenate(
            [z[:, n * A_HEAD_DIM:(n + 1) * A_HEAD_DIM] for n in range(n_blocks)], axis=0)
        out_a_heads.append(u[:, cols] * z_rows)
    out_a = jnp.concatenate(out_a_heads, axis=1)

    @pl.when(s == 0)
    def _():
        pbuf_ref[0:POOL_HALO, :] = jnp.zeros((POOL_HALO, B_WIDTH), jnp.float32)

    pbuf_ref[POOL_HALO:POOL_HALO + tile, :] = proj[:, 2 * A_WIDTH:]
    t = s * tile + lax.broadcasted_iota(jnp.int32, (tile, 1), 0)
    out_b_groups = []
    for g, w in enumerate(POOL_WINDOWS):
        cols = slice(g * B_GROUP_DIM, (g + 1) * B_GROUP_DIM)
        cur = pbuf_ref[POOL_HALO:POOL_HALO + tile, cols]
        win = cur
        for k in range(1, w):
            win = win + pbuf_ref[POOL_HALO - k:POOL_HALO - k + tile, cols]
        inv_cnt = 1.0 / jnp.minimum(t + 1, w).astype(jnp.float32)
        pooled = win * inv_cnt - cur
        out_b_groups.append(_bf16_dot(pooled, w_pool_ref[g]))
    out_b = jnp.concatenate(out_b_groups, axis=1) * pool_scale_ref[...]
    pbuf_ref[0:POOL_HALO, :] = pbuf_ref[tile:tile + POOL_HALO, :]

    mixed = _bf16_dot(jnp.concatenate([out_a, out_b], axis=1), w_out_ref[...])
    o_ref[...] = x + g_m * (_rms_scale(mixed) * post_g_ref[...])


def _mixing(x, mod, pre_g, post_g, w_in, sgu_g, w_sp, b_sp_t, w_pool, pool_scale, w_out):
    bsz, seq, d = x.shape
    tile = SEQ_TILE_MIX
    const2 = lambda b, s: (0, 0)
    const3 = lambda b, s: (0, 0, 0)
    resident = dict(pipeline_mode=pl.Buffered(1))
    return pl.pallas_call(
        _mix_kernel,
        grid=(bsz, seq // tile),
        in_specs=[
            pl.BlockSpec((None, tile, d), lambda b, s: (b, s, 0)),
            pl.BlockSpec((None, N_MOD, d), lambda b, s: (b, 0, 0)),
            pl.BlockSpec(pre_g.shape, const2, **resident),
            pl.BlockSpec(post_g.shape, const2, **resident),
            pl.BlockSpec(w_in.shape, const2, **resident),
            pl.BlockSpec(sgu_g.shape, const2, **resident),
            pl.BlockSpec(w_sp.shape, const3, **resident),
            pl.BlockSpec(b_sp_t.shape, const2, **resident),
            pl.BlockSpec(w_pool.shape, const3, **resident),
            pl.BlockSpec(pool_scale.shape, const2, **resident),
            pl.BlockSpec(w_out.shape, const2, **resident),
        ],
        out_specs=pl.BlockSpec((None, tile, d), lambda b, s: (b, s, 0)),
        out_shape=jax.ShapeDtypeStruct(x.shape, x.dtype),
        scratch_shapes=[pltpu.VMEM((POOL_HALO + tile, B_WIDTH), jnp.float32)],
        compiler_params=pltpu.CompilerParams(dimension_semantics=("arbitrary", "arbitrary"),
                                             vmem_limit_bytes=VMEM_LIMIT_BYTES),
        name="mixing_sublayer",
    )(x, mod, pre_g, post_g, w_in, sgu_g, w_sp, b_sp_t, w_pool, pool_scale, w_out)


def _ffn_kernel(x_ref, mod_ref, pre_g_ref, post_g_ref, w_up_ref, conv_w_ref, conv_b_ref, w_down_ref,
                o_ref, h_ref, ubuf_ref, carry_ref, acc_ref):
    s = pl.program_id(1)
    tile = x_ref.shape[0]
    n_chunks = w_up_ref.shape[0]
    fc = w_down_ref.shape[1]

    x = x_ref[...]
    mod = mod_ref[...]
    sh_f, sc_f, g_f = mod[3:4], mod[4:5], mod[5:6]
    h_ref[...] = (_rms_scale(x) * pre_g_ref[...] * (1.0 + sc_f) + sh_f).astype(jnp.bfloat16)

    @pl.when(s == 0)
    def _():
        carry_ref[...] = jnp.zeros(carry_ref.shape, jnp.float32)

    for c in range(n_chunks):
        up = jnp.dot(h_ref[...], w_up_ref[c], preferred_element_type=jnp.float32)
        ubuf_ref[0:CONV_HALO, :] = carry_ref[c]
        ubuf_ref[CONV_HALO:CONV_HALO + tile, :] = up
        carry_ref[c] = ubuf_ref[tile:tile + CONV_HALO, :]
        cw = conv_w_ref[c]
        y = conv_b_ref[c]
        for k in range(CONV_W):
            lo = CONV_HALO - (CONV_W - 1) + k
            y = y + ubuf_ref[lo:lo + tile, :] * cw[k:k + 1]
        act = (_silu(y[:, :fc]) * y[:, fc:]).astype(jnp.bfloat16)
        part = jnp.dot(act, w_down_ref[c], preferred_element_type=jnp.float32)
        if c == 0:
            acc_ref[...] = part
        else:
            acc_ref[...] += part

    f = acc_ref[...]
    o_ref[...] = x + g_f * (_rms_scale(f) * post_g_ref[...])


def _channel(x, mod, pre_g, post_g, w_up_c, conv_w_c, conv_b_c, w_down_c):
    bsz, seq, d = x.shape
    tile = SEQ_TILE_FFN
    n_chunks, _, two_fc = w_up_c.shape
    const2 = lambda b, s: (0, 0)
    const3 = lambda b, s: (0, 0, 0)
    resident = dict(pipeline_mode=pl.Buffered(1))
    return pl.pallas_call(
        _ffn_kernel,
        grid=(bsz, seq // tile),
        in_specs=[
            pl.BlockSpec((None, tile, d), lambda b, s: (b, s, 0)),
            pl.BlockSpec((None, N_MOD, d), lambda b, s: (b, 0, 0)),
            pl.BlockSpec(pre_g.shape, const2, **resident),
            pl.BlockSpec(post_g.shape, const2, **resident),
            pl.BlockSpec(w_up_c.shape, const3, **resident),
            pl.BlockSpec(conv_w_c.shape, const3, **resident),
            pl.BlockSpec(conv_b_c.shape, const3, **resident),
            pl.BlockSpec(w_down_c.shape, const3, **resident),
        ],
        out_specs=pl.BlockSpec((None, tile, d), lambda b, s: (b, s, 0)),
        out_shape=jax.ShapeDtypeStruct(x.shape, x.dtype),
        scratch_shapes=[
            pltpu.VMEM((tile, d), jnp.bfloat16),
            pltpu.VMEM((CONV_HALO + tile, two_fc), jnp.float32),
            pltpu.VMEM((n_chunks, CONV_HALO, two_fc), jnp.float32),
            pltpu.VMEM((tile, d), jnp.float32),
        ],
        compiler_params=pltpu.CompilerParams(dimension_semantics=("arbitrary", "arbitrary"),
                                             vmem_limit_bytes=VMEM_LIMIT_BYTES),
        name="channel_sublayer",
    )(x, mod, pre_g, post_g, w_up_c, conv_w_c, conv_b_c, w_down_c)


def _chunk_ff_columns(a, d_ff):
    n_chunks = d_ff // FF_CHUNK
    lead = a.shape[:-1]
    gate = a[..., :d_ff].reshape(*lead, n_chunks, FF_CHUNK)
    val = a[..., d_ff:].reshape(*lead, n_chunks, FF_CHUNK)
    both = jnp.concatenate([gate, val], axis=-1)
    return jnp.moveaxis(both, -2, 0)


def kernel(x, c, w_ada, b_ada, pre_mix_g, post_mix_g, w_in, sgu_norm_g, w_spatial, b_spatial, w_pool,
           pool_scale, w_out, pre_ffn_g, post_ffn_g, w_up, conv_w, conv_b, w_down):
    depth = w_ada.shape[0]
    bsz, seq, d = x.shape
    d_ff = w_down.shape[1]
    assert seq % SEQ_TILE_MIX == 0 and seq % SEQ_TILE_FFN == 0
    assert SEQ_TILE_MIX % SGU_BLOCK == 0 and d_ff % FF_CHUNK == 0
    bf16 = jnp.bfloat16
    for l in range(depth):
        mod = _modulation(c, w_ada[l], b_ada[l])
        x = _mixing(
            x, mod, pre_mix_g[l].reshape(1, d), post_mix_g[l].reshape(1, d), w_in[l].astype(bf16),
            sgu_norm_g[l].reshape(1, A_WIDTH), w_spatial[l], b_spatial[l].T, w_pool[l].astype(bf16),
            pool_scale[l].reshape(1, B_WIDTH), w_out[l].astype(bf16))
        x = _channel(
            x, mod, pre_ffn_g[l].reshape(1, d), post_ffn_g[l].reshape(1, d),
            _chunk_ff_columns(w_up[l], d_ff).astype(bf16),
            _chunk_ff_columns(conv_w[l], d_ff),
            _chunk_ff_columns(conv_b[l].reshape(1, -1), d_ff),
            w_down[l].reshape(d_ff // FF_CHUNK, FF_CHUNK, d).astype(bf16))
    return x
```

```python
import functools

import jax
import jax.numpy as jnp
from jax import lax
from jax.experimental import pallas as pl
from jax.experimental.pallas import tpu as pltpu

CHUNK = 64
A_HEADS = 4
A_HEAD_DIM = 128
A_WIDTH = A_HEADS * A_HEAD_DIM
SGU_BLOCK = 128
B_GROUPS = 4
B_GROUP_DIM = 128
B_WIDTH = B_GROUPS * B_GROUP_DIM
POOL_WINDOWS = (2, 4, 8, 16)
CONV_W = 3
N_MOD = 6
EPS = 1e-6

SUBLANES = 8
LANES = 128

POOL_HALO = 16
DOWN_GROUP = 2
SEQ_TILE_MIX = 256
SEQ_TILE_FFN = 256
FF_CHUNK = 256
MOD_COLS = 1536
VMEM_LIMIT_BYTES = 56 * 1024 * 1024


def _rms_scale(x):
    return x * lax.rsqrt(jnp.mean(x * x, axis=-1, keepdims=True) + EPS)


def _gelu_tanh(x):
    c = 0.7978845608028654
    return x * (0.5 * (1.0 + jnp.tanh(c * (x + 0.044715 * (x * x * x)))))


def _silu(x):
    return x * jax.nn.sigmoid(x)


def _bf16_dot(a, b):
    return jnp.dot(a.astype(jnp.bfloat16), b.astype(jnp.bfloat16), preferred_element_type=jnp.float32)


def _mod_kernel(c_ref, w_ref, b_ref, o_ref):
    sc = _silu(c_ref[...])
    o_ref[...] = jnp.dot(sc, w_ref[...], preferred_element_type=jnp.float32) + b_ref[...]


def _modulation(c, w_ada, b_ada):
    bsz, d = c.shape
    n = w_ada.shape[1]
    rows = -(-bsz // SUBLANES) * SUBLANES
    c_pad = jnp.pad(c, ((0, rows - bsz), (0, 0)))
    out = pl.pallas_call(
        _mod_kernel,
        grid=(n // MOD_COLS,),
        in_specs=[
            pl.BlockSpec((rows, d), lambda j: (0, 0)),
            pl.BlockSpec((d, MOD_COLS), lambda j: (0, j)),
            pl.BlockSpec((1, MOD_COLS), lambda j: (0, j)),
        ],
        out_specs=pl.BlockSpec((rows, MOD_COLS), lambda j: (0, j)),
        out_shape=jax.ShapeDtypeStruct((rows, n), jnp.float32),
        compiler_params=pltpu.CompilerParams(dimension_semantics=("arbitrary",),
                                             vmem_limit_bytes=VMEM_LIMIT_BYTES),
        name="adaln_modulation",
    )(c_pad, w_ada, b_ada.reshape(1, n))
    return out[:bsz].reshape(bsz, N_MOD, d)


def _mix_kernel(x_ref, mod_ref, pre_g_ref, post_g_ref, w_in_ref, sgu_g_ref, w_sp_ref, b_sp_ref,
                w_pool_ref, pool_scale_ref, w_out_ref, o_ref, pbuf_ref):
    s = pl.program_id(1)
    tile = x_ref.shape[0]
    n_blocks = tile // SGU_BLOCK

    x = x_ref[...]
    mod = mod_ref[...]
    sh_m, sc_m, g_m = mod[0:1], mod[1:2], mod[2:3]

    h = _rms_scale(x) * pre_g_ref[...] * (1.0 + sc_m) + sh_m
    proj = _bf16_dot(h, w_in_ref[...])

    a = _gelu_tanh(proj[:, :2 * A_WIDTH])
    u, v = a[:, :A_WIDTH], a[:, A_WIDTH:]
    row = lax.broadcasted_iota(jnp.int32, (SGU_BLOCK, SGU_BLOCK), 0)
    col = lax.broadcasted_iota(jnp.int32, (SGU_BLOCK, SGU_BLOCK), 1)
    causal = (col // CHUNK) <= (row // CHUNK)
    sgu_g = sgu_g_ref[...]
    b_sp = b_sp_ref[...]
    out_a_heads = []
    for hd in range(A_HEADS):
        cols = slice(hd * A_HEAD_DIM, (hd + 1) * A_HEAD_DIM)
        vn = (_rms_scale(v[:, cols]) * sgu_g[:, cols]).astype(jnp.bfloat16)
        v_cat = jnp.concatenate(
            [vn[n * SGU_BLOCK:(n + 1) * SGU_BLOCK] for n in range(n_blocks)], axis=1)
        w_m = jnp.where(causal, w_sp_ref[hd], 0.0)
        z = _bf16_dot(w_m, v_cat) + b_sp[:, hd:hd + 1]
        z_rows = jnp.concatenate(
            [z[:, n * A_HEAD_DIM:(n + 1) * A_HEAD_DIM] for n in range(n_blocks)], axis=0)
        out_a_heads.append(u[:, cols] * z_rows)
    out_a = jnp.concatenate(out_a_heads, axis=1)

    @pl.when(s == 0)
    def _():
        pbuf_ref[0:POOL_HALO, :] = jnp.zeros((POOL_HALO, B_WIDTH), jnp.float32)

    pbuf_ref[POOL_HALO:POOL_HALO + tile, :] = proj[:, 2 * A_WIDTH:]
    t = s * tile + lax.broadcasted_iota(jnp.int32, (tile, 1), 0)
    out_b_groups = []
    for g, w in enumerate(POOL_WINDOWS):
        cols = slice(g * B_GROUP_DIM, (g + 1) * B_GROUP_DIM)
        cur = pbuf_ref[POOL_HALO:POOL_HALO + tile, cols]
        win = cur
        for k in range(1, w):
            win = win + pbuf_ref[POOL_HALO - k:POOL_HALO - k + tile, cols]
        inv_cnt = 1.0 / jnp.minimum(t + 1, w).astype(jnp.float32)
        pooled = win * inv_cnt - cur
        out_b_groups.append(_bf16_dot(pooled, w_pool_ref[g]))
    out_b = jnp.concatenate(out_b_groups, axis=1) * pool_scale_ref[...]
    pbuf_ref[0:POOL_HALO, :] = pbuf_ref[tile:tile + POOL_HALO, :]

    mixed = _bf16_dot(jnp.concatenate([out_a, out_b], axis=1), w_out_ref[...])
    o_ref[...] = x + g_m * (_rms_scale(mixed) * post_g_ref[...])


def _mixing(x, mod, pre_g, post_g, w_in, sgu_g, w_sp, b_sp_t, w_pool, pool_scale, w_out):
    bsz, seq, d = x.shape
    tile = SEQ_TILE_MIX
    const2 = lambda b, s: (0, 0)
    const3 = lambda b, s: (0, 0, 0)
    resident = dict(pipeline_mode=pl.Buffered(1))
    return pl.pallas_call(
        _mix_kernel,
        grid=(bsz, seq // tile),
        in_specs=[
            pl.BlockSpec((None, tile, d), lambda b, s: (b, s, 0)),
            pl.BlockSpec((None, N_MOD, d), lambda b, s: (b, 0, 0)),
            pl.BlockSpec(pre_g.shape, const2, **resident),
            pl.BlockSpec(post_g.shape, const2, **resident),
            pl.BlockSpec(w_in.shape, const2, **resident),
            pl.BlockSpec(sgu_g.shape, const2, **resident),
            pl.BlockSpec(w_sp.shape, const3, **resident),
            pl.BlockSpec(b_sp_t.shape, const2, **resident),
            pl.BlockSpec(w_pool.shape, const3, **resident),
            pl.BlockSpec(pool_scale.shape, const2, **resident),
            pl.BlockSpec(w_out.shape, const2, **resident),
        ],
        out_specs=pl.BlockSpec((None, tile, d), lambda b, s: (b, s, 0)),
        out_shape=jax.ShapeDtypeStruct(x.shape, x.dtype),
        scratch_shapes=[pltpu.VMEM((POOL_HALO + tile, B_WIDTH), jnp.float32)],
        compiler_params=pltpu.CompilerParams(dimension_semantics=("arbitrary", "arbitrary"),
                                             vmem_limit_bytes=VMEM_LIMIT_BYTES),
        name="mixing_sublayer",
    )(x, mod, pre_g, post_g, w_in, sgu_g, w_sp, b_sp_t, w_pool, pool_scale, w_out)


def _time_strided_permutation(tile):
    n_blocks = tile // SUBLANES
    p = jnp.arange(tile)
    src = (p % SUBLANES) * n_blocks + p // SUBLANES
    return (src[:, None] == jnp.arange(tile)[None, :]).astype(jnp.bfloat16)


def _shift_rows_in(prev_blk, cur_blk):
    sub = lax.broadcasted_iota(jnp.int32, cur_blk.shape, 0)
    return jnp.where(sub == 0, pltpu.roll(prev_blk, 1, axis=0), pltpu.roll(cur_blk, 1, axis=0))


def _ffn_kernel(x_ref, mod_ref, perm_ref, pre_g_ref, post_g_ref, w_up_ref, conv_w_ref, conv_b_ref,
                w_down_ref, o_ref, h_ref, carry_ref, stage_ref):
    s = pl.program_id(1)
    tile, d = x_ref.shape
    n_blocks = tile // SUBLANES
    n_chunks = w_up_ref.shape[0]
    fc = w_up_ref.shape[2] // 2

    x = x_ref[...]
    mod = mod_ref[...]
    sh_f, sc_f, g_f = mod[3:4], mod[4:5], mod[5:6]
    h = (_rms_scale(x) * pre_g_ref[...] * (1.0 + sc_f) + sh_f).astype(jnp.bfloat16)
    h_ref[...] = jnp.dot(perm_ref[...], h, preferred_element_type=jnp.float32).astype(jnp.bfloat16)

    @pl.when(s == 0)
    def _():
        carry_ref[...] = jnp.zeros(carry_ref.shape, jnp.float32)

    def conv_gate(c, up):
        prev = carry_ref[c]
        carry_ref[c] = up[tile - 2 * SUBLANES:, :]
        back1 = _shift_rows_in(prev[SUBLANES:], up[tile - SUBLANES:])
        back2 = _shift_rows_in(prev[:SUBLANES], up[tile - 2 * SUBLANES:tile - SUBLANES])
        up1 = jnp.concatenate([back1, up[:tile - SUBLANES]], axis=0)
        up2 = jnp.concatenate([back2, back1, up[:tile - 2 * SUBLANES]], axis=0)
        cw = conv_w_ref[c]
        y = conv_b_ref[c] + up2 * cw[0:1] + up1 * cw[1:2] + up * cw[2:3]
        return (_silu(y[:, :fc]) * y[:, fc:]).astype(jnp.bfloat16)

    def up_project(c):
        return jnp.dot(h_ref[...], w_up_ref[c], preferred_element_type=jnp.float32)

    f = None
    acts = []
    up_next = up_project(0)
    for c in range(n_chunks):
        up = up_next
        if c + 1 < n_chunks:
            up_next = up_project(c + 1)
        if len(acts) == DOWN_GROUP:
            lo = (c - DOWN_GROUP) * fc
            part = jnp.dot(jnp.concatenate(acts, axis=1), w_down_ref[lo:lo + DOWN_GROUP * fc, :],
                           preferred_element_type=jnp.float32)
            f = part if f is None else f + part
            acts = []
        acts.append(conv_gate(c, up))
    lo = (n_chunks - len(acts)) * fc
    part = jnp.dot(jnp.concatenate(acts, axis=1), w_down_ref[lo:, :], preferred_element_type=jnp.float32)
    f = part if f is None else f + part

    delta = g_f * (_rms_scale(f) * post_g_ref[...])
    blocks_per_segment = n_blocks // SUBLANES
    for j in range(d // LANES):
        cols = slice(j * LANES, (j + 1) * LANES)
        stage_ref[j] = delta[:, cols]
        for q in range(n_blocks):
            start = SUBLANES * SUBLANES * (q % blocks_per_segment) + q // blocks_per_segment
            rows = slice(q * SUBLANES, (q + 1) * SUBLANES)
            o_ref[rows, cols] = x_ref[rows, cols] + stage_ref[j, pl.ds(start, SUBLANES, stride=SUBLANES), :]


def _channel(x, mod, pre_g, post_g, w_up_c, conv_w_c, conv_b_c, w_down):
    bsz, seq, d = x.shape
    tile = SEQ_TILE_FFN
    n_chunks, _, two_fc = w_up_c.shape
    perm = _time_strided_permutation(tile)
    const2 = lambda b, s: (0, 0)
    const3 = lambda b, s: (0, 0, 0)
    resident = dict(pipeline_mode=pl.Buffered(1))
    return pl.pallas_call(
        _ffn_kernel,
        grid=(bsz, seq // tile),
        in_specs=[
            pl.BlockSpec((None, tile, d), lambda b, s: (b, s, 0)),
            pl.BlockSpec((None, N_MOD, d), lambda b, s: (b, 0, 0)),
            pl.BlockSpec(perm.shape, const2, **resident),
            pl.BlockSpec(pre_g.shape, const2, **resident),
            pl.BlockSpec(post_g.shape, const2, **resident),
            pl.BlockSpec(w_up_c.shape, const3, **resident),
            pl.BlockSpec(conv_w_c.shape, const3, **resident),
            pl.BlockSpec(conv_b_c.shape, const3, **resident),
            pl.BlockSpec(w_down.shape, const2, **resident),
        ],
        out_specs=pl.BlockSpec((None, tile, d), lambda b, s: (b, s, 0)),
        out_shape=jax.ShapeDtypeStruct(x.shape, x.dtype),
        scratch_shapes=[
            pltpu.VMEM((tile, d), jnp.bfloat16),
            pltpu.VMEM((n_chunks, 2 * SUBLANES, two_fc), jnp.float32),
            pltpu.VMEM((d // LANES, tile, LANES), jnp.float32),
        ],
        compiler_params=pltpu.CompilerParams(dimension_semantics=("arbitrary", "arbitrary"),
                                             vmem_limit_bytes=VMEM_LIMIT_BYTES),
        name="channel_sublayer",
    )(x, mod, perm, pre_g, post_g, w_up_c, conv_w_c, conv_b_c, w_down)


def _chunk_ff_columns(a, d_ff):
    n_chunks = d_ff // FF_CHUNK
    lead = a.shape[:-1]
    gate = a[..., :d_ff].reshape(*lead, n_chunks, FF_CHUNK)
    val = a[..., d_ff:].reshape(*lead, n_chunks, FF_CHUNK)
    both = jnp.concatenate([gate, val], axis=-1)
    return jnp.moveaxis(both, -2, 0)


def kernel(x, c, w_ada, b_ada, pre_mix_g, post_mix_g, w_in, sgu_norm_g, w_spatial, b_spatial, w_pool,
           pool_scale, w_out, pre_ffn_g, post_ffn_g, w_up, conv_w, conv_b, w_down):
    depth = w_ada.shape[0]
    bsz, seq, d = x.shape
    d_ff = w_down.shape[1]
    assert seq % SEQ_TILE_MIX == 0 and seq % SEQ_TILE_FFN == 0
    assert SEQ_TILE_MIX % SGU_BLOCK == 0 and d_ff % FF_CHUNK == 0
    bf16 = jnp.bfloat16
    for l in range(depth):
        mod = _modulation(c, w_ada[l], b_ada[l])
        x = _mixing(
            x, mod, pre_mix_g[l].reshape(1, d), post_mix_g[l].reshape(1, d), w_in[l].astype(bf16),
            sgu_norm_g[l].reshape(1, A_WIDTH), w_spatial[l], b_spatial[l].T, w_pool[l].astype(bf16),
            pool_scale[l].reshape(1, B_WIDTH), w_out[l].astype(bf16))
        x = _channel(
            x, mod, pre_ffn_g[l].reshape(1, d), post_ffn_g[l].reshape(1, d),
            _chunk_ff_columns(w_up[l], d_ff).astype(bf16),
            _chunk_ff_columns(conv_w[l], d_ff),
            _chunk_ff_columns(conv_b[l].reshape(1, -1), d_ff),
            w_down[l].astype(bf16))
    return x
```

```python
import functools

import jax
import jax.numpy as jnp
from jax import lax
from jax.experimental import pallas as pl
from jax.experimental.pallas import tpu as pltpu

CHUNK = 64
A_HEADS = 4
A_HEAD_DIM = 128
A_WIDTH = A_HEADS * A_HEAD_DIM
SGU_BLOCK = 128
B_GROUPS = 4
B_GROUP_DIM = 128
B_WIDTH = B_GROUPS * B_GROUP_DIM
POOL_WINDOWS = (2, 4, 8, 16)
CONV_W = 3
N_MOD = 6
EPS = 1e-6

SUBLANES = 8
LANES = 128

POOL_HALO = 16
DOWN_GROUP = 2
SEQ_TILE_MIX = 1024
SUB_TILE_MIX = 256
SEQ_TILE_FFN = 256
FF_CHUNK = 256
MOD_COLS = 1536
VMEM_LIMIT_BYTES = 56 * 1024 * 1024


def _rms_scale(x):
    return x * lax.rsqrt(jnp.mean(x * x, axis=-1, keepdims=True) + EPS)


def _gelu_tanh(x):
    c = 0.7978845608028654
    half = 0.5 * x
    return half * jnp.tanh(x * ((c * 0.044715) * (x * x) + c)) + half


def _silu(x):
    return x * jax.nn.sigmoid(x)


def _bf16_dot(a, b):
    return jnp.dot(a.astype(jnp.bfloat16), b.astype(jnp.bfloat16), preferred_element_type=jnp.float32)


def _mod_kernel(c_ref, w_ref, b_ref, o_ref):
    sc = _silu(c_ref[...])
    o_ref[...] = jnp.dot(sc, w_ref[...], preferred_element_type=jnp.float32) + b_ref[...]


def _modulation(c, w_ada, b_ada):
    bsz, d = c.shape
    n = w_ada.shape[1]
    rows = -(-bsz // SUBLANES) * SUBLANES
    c_pad = jnp.pad(c, ((0, rows - bsz), (0, 0)))
    out = pl.pallas_call(
        _mod_kernel,
        grid=(n // MOD_COLS,),
        in_specs=[
            pl.BlockSpec((rows, d), lambda j: (0, 0)),
            pl.BlockSpec((d, MOD_COLS), lambda j: (0, j)),
            pl.BlockSpec((1, MOD_COLS), lambda j: (0, j)),
        ],
        out_specs=pl.BlockSpec((rows, MOD_COLS), lambda j: (0, j)),
        out_shape=jax.ShapeDtypeStruct((rows, n), jnp.float32),
        compiler_params=pltpu.CompilerParams(dimension_semantics=("arbitrary",),
                                             vmem_limit_bytes=VMEM_LIMIT_BYTES),
        name="adaln_modulation",
    )(c_pad, w_ada, b_ada.reshape(1, n))
    return out[:bsz].reshape(bsz, N_MOD, d)


def _mix_kernel(x_ref, mod_ref, pre_g_ref, post_g_ref, w_in_ref, sgu_g_ref, w_sp_ref, b_sp_ref,
                w_pool_ref, pool_scale_ref, w_out_ref, o_ref, pbuf_ref):
    s = pl.program_id(1)
    tile = x_ref.shape[0]
    sub = SUB_TILE_MIX
    n_sub = tile // sub
    n_blocks = sub // SGU_BLOCK

    mod = mod_ref[...]
    sh_m, sc_m, g_m = mod[0:1], mod[1:2], mod[2:3]
    in_gain = pre_g_ref[...] * (1.0 + sc_m)
    out_gain = post_g_ref[...] * g_m
    row = lax.broadcasted_iota(jnp.int32, (SGU_BLOCK, SGU_BLOCK), 0)
    col = lax.broadcasted_iota(jnp.int32, (SGU_BLOCK, SGU_BLOCK), 1)
    causal = (col // CHUNK) <= (row // CHUNK)
    w_m = [jnp.where(causal, w_sp_ref[hd], 0.0).astype(jnp.bfloat16) for hd in range(A_HEADS)]
    sgu_g = sgu_g_ref[...]
    b_sp = b_sp_ref[...]

    @pl.when(s == 0)
    def _():
        pbuf_ref[0:POOL_HALO, :] = jnp.zeros((POOL_HALO, B_WIDTH), jnp.float32)

    def project_in(i):
        x = x_ref[i * sub:(i + 1) * sub, :]
        h = _rms_scale(x) * in_gain + sh_m
        return _bf16_dot(h, w_in_ref[...])

    def mix_heads(i, proj):
        a = _gelu_tanh(proj[:, :2 * A_WIDTH])
        u, v = a[:, :A_WIDTH], a[:, A_WIDTH:]
        out_a_heads = []
        for hd in range(A_HEADS):
            cols = slice(hd * A_HEAD_DIM, (hd + 1) * A_HEAD_DIM)
            vn = (_rms_scale(v[:, cols]) * sgu_g[:, cols]).astype(jnp.bfloat16)
            v_cat = jnp.concatenate(
                [vn[n * SGU_BLOCK:(n + 1) * SGU_BLOCK] for n in range(n_blocks)], axis=1)
            z = jnp.dot(w_m[hd], v_cat, preferred_element_type=jnp.float32) + b_sp[:, hd:hd + 1]
            z_rows = jnp.concatenate(
                [z[:, n * A_HEAD_DIM:(n + 1) * A_HEAD_DIM] for n in range(n_blocks)], axis=0)
            out_a_heads.append(u[:, cols] * z_rows)

        base = POOL_HALO + i * sub
        pbuf_ref[base:base + sub, :] = proj[:, 2 * A_WIDTH:]
        t = s * tile + i * sub + lax.broadcasted_iota(jnp.int32, (sub, 1), 0)
        out_b_groups = []
        for g, w in enumerate(POOL_WINDOWS):
            cols = slice(g * B_GROUP_DIM, (g + 1) * B_GROUP_DIM)
            cur = pbuf_ref[base:base + sub, cols]
            win = cur
            for k in range(1, w):
                win = win + pbuf_ref[base - k:base - k + sub, cols]
            inv_cnt = 1.0 / jnp.minimum(t + 1, w).astype(jnp.float32)
            pooled = win * inv_cnt - cur
            out_b_groups.append(_bf16_dot(pooled, w_pool_ref[g]))
        out_b = jnp.concatenate(out_b_groups, axis=1) * pool_scale_ref[...]
        return jnp.concatenate(out_a_heads + [out_b], axis=1).astype(jnp.bfloat16)

    def finish(i, mixed):
        rows = slice(i * sub, (i + 1) * sub)
        o_ref[rows, :] = x_ref[rows, :] + _rms_scale(mixed) * out_gain

    proj_next = project_in(0)
    pending = None
    for i in range(n_sub):
        proj = proj_next
        if i + 1 < n_sub:
            proj_next = project_in(i + 1)
        mixed = jnp.dot(mix_heads(i, proj), w_out_ref[...], preferred_element_type=jnp.float32)
        if pending is not None:
            finish(*pending)
        pending = (i, mixed)
    finish(*pending)
    pbuf_ref[0:POOL_HALO, :] = pbuf_ref[tile:tile + POOL_HALO, :]


def _mixing(x, mod, pre_g, post_g, w_in, sgu_g, w_sp, b_sp_t, w_pool, pool_scale, w_out):
    bsz, seq, d = x.shape
    tile = SEQ_TILE_MIX
    const2 = lambda b, s: (0, 0)
    const3 = lambda b, s: (0, 0, 0)
    resident = dict(pipeline_mode=pl.Buffered(1))
    return pl.pallas_call(
        _mix_kernel,
        grid=(bsz, seq // tile),
        in_specs=[
            pl.BlockSpec((None, tile, d), lambda b, s: (b, s, 0)),
            pl.BlockSpec((None, N_MOD, d), lambda b, s: (b, 0, 0)),
            pl.BlockSpec(pre_g.shape, const2, **resident),
            pl.BlockSpec(post_g.shape, const2, **resident),
            pl.BlockSpec(w_in.shape, const2, **resident),
            pl.BlockSpec(sgu_g.shape, const2, **resident),
            pl.BlockSpec(w_sp.shape, const3, **resident),
            pl.BlockSpec(b_sp_t.shape, const2, **resident),
            pl.BlockSpec(w_pool.shape, const3, **resident),
            pl.BlockSpec(pool_scale.shape, const2, **resident),
            pl.BlockSpec(w_out.shape, const2, **resident),
        ],
        out_specs=pl.BlockSpec((None, tile, d), lambda b, s: (b, s, 0)),
        out_shape=jax.ShapeDtypeStruct(x.shape, x.dtype),
        scratch_shapes=[pltpu.VMEM((POOL_HALO + tile, B_WIDTH), jnp.float32)],
        compiler_params=pltpu.CompilerParams(dimension_semantics=("arbitrary", "arbitrary"),
                                             vmem_limit_bytes=VMEM_LIMIT_BYTES),
        name="mixing_sublayer",
    )(x, mod, pre_g, post_g, w_in, sgu_g, w_sp, b_sp_t, w_pool, pool_scale, w_out)


def _time_strided_permutation(tile):
    n_blocks = tile // SUBLANES
    p = jnp.arange(tile)
    src = (p % SUBLANES) * n_blocks + p // SUBLANES
    return (src[:, None] == jnp.arange(tile)[None, :]).astype(jnp.bfloat16)


def _shift_rows_in(prev_blk, cur_blk):
    sub = lax.broadcasted_iota(jnp.int32, cur_blk.shape, 0)
    return jnp.where(sub == 0, pltpu.roll(prev_blk, 1, axis=0), pltpu.roll(cur_blk, 1, axis=0))


def _ffn_kernel(x_ref, mod_ref, perm_ref, pre_g_ref, post_g_ref, w_up_ref, conv_w_ref, conv_b_ref,
                w_down_ref, o_ref, h_ref, carry_ref, stage_ref):
    s = pl.program_id(1)
    tile, d = x_ref.shape
    n_blocks = tile // SUBLANES
    n_chunks = w_up_ref.shape[0]
    fc = w_up_ref.shape[2] // 2

    x = x_ref[...]
    mod = mod_ref[...]
    sh_f, sc_f, g_f = mod[3:4], mod[4:5], mod[5:6]
    h = (_rms_scale(x) * pre_g_ref[...] * (1.0 + sc_f) + sh_f).astype(jnp.bfloat16)
    h_ref[...] = jnp.dot(perm_ref[...], h, preferred_element_type=jnp.float32).astype(jnp.bfloat16)

    @pl.when(s == 0)
    def _():
        carry_ref[...] = jnp.zeros(carry_ref.shape, jnp.float32)

    def conv_gate(c, up):
        prev = carry_ref[c]
        carry_ref[c] = up[tile - 2 * SUBLANES:, :]
        back1 = _shift_rows_in(prev[SUBLANES:], up[tile - SUBLANES:])
        back2 = _shift_rows_in(prev[:SUBLANES], up[tile - 2 * SUBLANES:tile - SUBLANES])
        up1 = jnp.concatenate([back1, up[:tile - SUBLANES]], axis=0)
        up2 = jnp.concatenate([back2, back1, up[:tile - 2 * SUBLANES]], axis=0)
        cw = conv_w_ref[c]
        y = conv_b_ref[c] + up2 * cw[0:1] + up1 * cw[1:2] + up * cw[2:3]
        return (_silu(y[:, :fc]) * y[:, fc:]).astype(jnp.bfloat16)

    def up_project(c):
        return jnp.dot(h_ref[...], w_up_ref[c], preferred_element_type=jnp.float32)

    f = None
    acts = []
    up_next = up_project(0)
    for c in range(n_chunks):
        up = up_next
        if c + 1 < n_chunks:
            up_next = up_project(c + 1)
        if len(acts) == DOWN_GROUP:
            lo = (c - DOWN_GROUP) * fc
            part = jnp.dot(jnp.concatenate(acts, axis=1), w_down_ref[lo:lo + DOWN_GROUP * fc, :],
                           preferred_element_type=jnp.float32)
            f = part if f is None else f + part
            acts = []
        acts.append(conv_gate(c, up))
    lo = (n_chunks - len(acts)) * fc
    part = jnp.dot(jnp.concatenate(acts, axis=1), w_down_ref[lo:, :], preferred_element_type=jnp.float32)
    f = part if f is None else f + part

    delta = g_f * (_rms_scale(f) * post_g_ref[...])
    blocks_per_segment = n_blocks // SUBLANES
    for j in range(d // LANES):
        cols = slice(j * LANES, (j + 1) * LANES)
        stage_ref[j] = delta[:, cols]
        for q in range(n_blocks):
            start = SUBLANES * SUBLANES * (q % blocks_per_segment) + q // blocks_per_segment
            rows = slice(q * SUBLANES, (q + 1) * SUBLANES)
            o_ref[rows, cols] = x_ref[rows, cols] + stage_ref[j, pl.ds(start, SUBLANES, stride=SUBLANES), :]


def _channel(x, mod, pre_g, post_g, w_up_c, conv_w_c, conv_b_c, w_down):
    bsz, seq, d = x.shape
    tile = SEQ_TILE_FFN
    n_chunks, _, two_fc = w_up_c.shape
    perm = _time_strided_permutation(tile)
    const2 = lambda b, s: (0, 0)
    const3 = lambda b, s: (0, 0, 0)
    resident = dict(pipeline_mode=pl.Buffered(1))
    return pl.pallas_call(
        _ffn_kernel,
        grid=(bsz, seq // tile),
        in_specs=[
            pl.BlockSpec((None, tile, d), lambda b, s: (b, s, 0)),
            pl.BlockSpec((None, N_MOD, d), lambda b, s: (b, 0, 0)),
            pl.BlockSpec(perm.shape, const2, **resident),
            pl.BlockSpec(pre_g.shape, const2, **resident),
            pl.BlockSpec(post_g.shape, const2, **resident),
            pl.BlockSpec(w_up_c.shape, const3, **resident),
            pl.BlockSpec(conv_w_c.shape, const3, **resident),
            pl.BlockSpec(conv_b_c.shape, const3, **resident),
            pl.BlockSpec(w_down.shape, const2, **resident),
        ],
        out_specs=pl.BlockSpec((None, tile, d), lambda b, s: (b, s, 0)),
        out_shape=jax.ShapeDtypeStruct(x.shape, x.dtype),
        scratch_shapes=[
            pltpu.VMEM((tile, d), jnp.bfloat16),
            pltpu.VMEM((n_chunks, 2 * SUBLANES, two_fc), jnp.float32),
            pltpu.VMEM((d // LANES, tile, LANES), jnp.float32),
        ],
        compiler_params=pltpu.CompilerParams(dimension_semantics=("arbitrary", "arbitrary"),
                                             vmem_limit_bytes=VMEM_LIMIT_BYTES),
        name="channel_sublayer",
    )(x, mod, perm, pre_g, post_g, w_up_c, conv_w_c, conv_b_c, w_down)


def _chunk_ff_columns(a, d_ff):
    n_chunks = d_ff // FF_CHUNK
    lead = a.shape[:-1]
    gate = a[..., :d_ff].reshape(*lead, n_chunks, FF_CHUNK)
    val = a[..., d_ff:].reshape(*lead, n_chunks, FF_CHUNK)
    both = jnp.concatenate([gate, val], axis=-1)
    return jnp.moveaxis(both, -2, 0)


def kernel(x, c, w_ada, b_ada, pre_mix_g, post_mix_g, w_in, sgu_norm_g, w_spatial, b_spatial, w_pool,
           pool_scale, w_out, pre_ffn_g, post_ffn_g, w_up, conv_w, conv_b, w_down):
    depth = w_ada.shape[0]
    bsz, seq, d = x.shape
    d_ff = w_down.shape[1]
    assert seq % SEQ_TILE_MIX == 0 and seq % SEQ_TILE_FFN == 0
    assert SEQ_TILE_MIX % SUB_TILE_MIX == 0 and SUB_TILE_MIX % SGU_BLOCK == 0 and d_ff % FF_CHUNK == 0
    bf16 = jnp.bfloat16
    for l in range(depth):
        mod = _modulation(c, w_ada[l], b_ada[l])
        x = _mixing(
            x, mod, pre_mix_g[l].reshape(1, d), post_mix_g[l].reshape(1, d), w_in[l].astype(bf16),
            sgu_norm_g[l].reshape(1, A_WIDTH), w_spatial[l], b_spatial[l].T, w_pool[l].astype(bf16),
            pool_scale[l].reshape(1, B_WIDTH), w_out[l].astype(bf16))
        x = _channel(
            x, mod, pre_ffn_g[l].reshape(1, d), post_ffn_g[l].reshape(1, d),
            _chunk_ff_columns(w_up[l], d_ff).astype(bf16),
            _chunk_ff_columns(conv_w[l], d_ff),
            _chunk_ff_columns(conv_b[l].reshape(1, -1), d_ff),
            w_down[l].astype(bf16))
    return x
```

```python
import functools

import jax
import jax.numpy as jnp
from jax import lax
from jax.experimental import pallas as pl
from jax.experimental.pallas import tpu as pltpu

CHUNK = 64
A_HEADS = 4
A_HEAD_DIM = 128
A_WIDTH = A_HEADS * A_HEAD_DIM
SGU_BLOCK = 128
B_GROUPS = 4
B_GROUP_DIM = 128
B_WIDTH = B_GROUPS * B_GROUP_DIM
POOL_WINDOWS = (2, 4, 8, 16)
CONV_W = 3
N_MOD = 6
EPS = 1e-6

SUBLANES = 8
LANES = 128

POOL_HALO = 16
DOWN_GROUP = 2
SEQ_TILE_MIX = 1024
SUB_TILE_MIX = 256
SEQ_TILE_FFN = 256
FF_CHUNK = 256
MOD_COLS = 1536
VMEM_LIMIT_BYTES = 56 * 1024 * 1024


def _rms_scale(x):
    return x * lax.rsqrt(jnp.mean(x * x, axis=-1, keepdims=True) + EPS)


def _gelu_tanh(x):
    c = 0.7978845608028654
    half = 0.5 * x
    return half * jnp.tanh(x * ((c * 0.044715) * (x * x) + c)) + half


def _silu(x):
    return x * jax.nn.sigmoid(x)


def _bf16_dot(a, b):
    return jnp.dot(a.astype(jnp.bfloat16), b.astype(jnp.bfloat16), preferred_element_type=jnp.float32)


def _mod_kernel(c_ref, w_ref, b_ref, o_ref):
    sc = _silu(c_ref[...])
    o_ref[...] = jnp.dot(sc, w_ref[...], preferred_element_type=jnp.float32) + b_ref[...]


def _modulation(c, w_ada, b_ada):
    bsz, d = c.shape
    n = w_ada.shape[1]
    rows = -(-bsz // SUBLANES) * SUBLANES
    c_pad = jnp.pad(c, ((0, rows - bsz), (0, 0)))
    out = pl.pallas_call(
        _mod_kernel,
        grid=(n // MOD_COLS,),
        in_specs=[
            pl.BlockSpec((rows, d), lambda j: (0, 0)),
            pl.BlockSpec((d, MOD_COLS), lambda j: (0, j)),
            pl.BlockSpec((1, MOD_COLS), lambda j: (0, j)),
        ],
        out_specs=pl.BlockSpec((rows, MOD_COLS), lambda j: (0, j)),
        out_shape=jax.ShapeDtypeStruct((rows, n), jnp.float32),
        compiler_params=pltpu.CompilerParams(dimension_semantics=("arbitrary",),
                                             vmem_limit_bytes=VMEM_LIMIT_BYTES),
        name="adaln_modulation",
    )(c_pad, w_ada, b_ada.reshape(1, n))
    return out[:bsz].reshape(bsz, N_MOD, d)


def _mix_kernel(x_ref, mod_ref, pre_g_ref, post_g_ref, w_in_ref, sgu_g_ref, w_sp_ref, b_sp_ref,
                w_pool_ref, pool_scale_ref, w_out_ref, o_ref, pbuf_ref):
    s = pl.program_id(1)
    tile = x_ref.shape[0]
    sub = SUB_TILE_MIX
    n_sub = tile // sub
    n_blocks = sub // SGU_BLOCK

    mod = mod_ref[...]
    sh_m, sc_m, g_m = mod[0:1], mod[1:2], mod[2:3]
    in_gain = pre_g_ref[...] * (1.0 + sc_m)
    out_gain = post_g_ref[...] * g_m
    row = lax.broadcasted_iota(jnp.int32, (SGU_BLOCK, SGU_BLOCK), 0)
    col = lax.broadcasted_iota(jnp.int32, (SGU_BLOCK, SGU_BLOCK), 1)
    causal = (col // CHUNK) <= (row // CHUNK)
    w_m = [jnp.where(causal, w_sp_ref[hd], 0.0).astype(jnp.bfloat16) for hd in range(A_HEADS)]
    sgu_g = sgu_g_ref[...]
    b_sp = b_sp_ref[...]

    @pl.when(s == 0)
    def _():
        pbuf_ref[0:POOL_HALO, :] = jnp.zeros((POOL_HALO, B_WIDTH), jnp.float32)

    def project_in(i):
        x = x_ref[i * sub:(i + 1) * sub, :]
        h = _rms_scale(x) * in_gain + sh_m
        return _bf16_dot(h, w_in_ref[...])

    def mix_heads(i, proj):
        a = _gelu_tanh(proj[:, :2 * A_WIDTH])
        u, v = a[:, :A_WIDTH], a[:, A_WIDTH:]
        out_a_heads = []
        for hd in range(A_HEADS):
            cols = slice(hd * A_HEAD_DIM, (hd + 1) * A_HEAD_DIM)
            vn = (_rms_scale(v[:, cols]) * sgu_g[:, cols]).astype(jnp.bfloat16)
            v_cat = jnp.concatenate(
                [vn[n * SGU_BLOCK:(n + 1) * SGU_BLOCK] for n in range(n_blocks)], axis=1)
            z = jnp.dot(w_m[hd], v_cat, preferred_element_type=jnp.float32) + b_sp[:, hd:hd + 1]
            z_rows = jnp.concatenate(
                [z[:, n * A_HEAD_DIM:(n + 1) * A_HEAD_DIM] for n in range(n_blocks)], axis=0)
            out_a_heads.append(u[:, cols] * z_rows)

        base = POOL_HALO + i * sub
        pbuf_ref[base:base + sub, :] = proj[:, 2 * A_WIDTH:]
        t = s * tile + i * sub + lax.broadcasted_iota(jnp.int32, (sub, 1), 0)
        out_b_groups = []
        for g, w in enumerate(POOL_WINDOWS):
            cols = slice(g * B_GROUP_DIM, (g + 1) * B_GROUP_DIM)
            cur = pbuf_ref[base:base + sub, cols]
            win = cur
            for k in range(1, w):
                win = win + pbuf_ref[base - k:base - k + sub, cols]
            inv_cnt = 1.0 / jnp.minimum(t + 1, w).astype(jnp.float32)
            pooled = win * inv_cnt - cur
            out_b_groups.append(_bf16_dot(pooled, w_pool_ref[g]))
        out_b = jnp.concatenate(out_b_groups, axis=1) * pool_scale_ref[...]
        return jnp.concatenate(out_a_heads + [out_b], axis=1).astype(jnp.bfloat16)

    def finish(i, mixed):
        rows = slice(i * sub, (i + 1) * sub)
        o_ref[rows, :] = x_ref[rows, :] + _rms_scale(mixed) * out_gain

    proj_next = project_in(0)
    pending = None
    for i in range(n_sub):
        proj = proj_next
        if i + 1 < n_sub:
            proj_next = project_in(i + 1)
        mixed = jnp.dot(mix_heads(i, proj), w_out_ref[...], preferred_element_type=jnp.float32)
        if pending is not None:
            finish(*pending)
        pending = (i, mixed)
    finish(*pending)
    pbuf_ref[0:POOL_HALO, :] = pbuf_ref[tile:tile + POOL_HALO, :]


def _mixing(x, mod, pre_g, post_g, w_in, sgu_g, w_sp, b_sp_t, w_pool, pool_scale, w_out):
    bsz, seq, d = x.shape
    tile = SEQ_TILE_MIX
    const2 = lambda b, s: (0, 0)
    const3 = lambda b, s: (0, 0, 0)
    resident = dict(pipeline_mode=pl.Buffered(1))
    return pl.pallas_call(
        _mix_kernel,
        grid=(bsz, seq // tile),
        in_specs=[
            pl.BlockSpec((None, tile, d), lambda b, s: (b, s, 0)),
            pl.BlockSpec((None, N_MOD, d), lambda b, s: (b, 0, 0)),
            pl.BlockSpec(pre_g.shape, const2, **resident),
            pl.BlockSpec(post_g.shape, const2, **resident),
            pl.BlockSpec(w_in.shape, const2, **resident),
            pl.BlockSpec(sgu_g.shape, const2, **resident),
            pl.BlockSpec(w_sp.shape, const3, **resident),
            pl.BlockSpec(b_sp_t.shape, const2, **resident),
            pl.BlockSpec(w_pool.shape, const3, **resident),
            pl.BlockSpec(pool_scale.shape, const2, **resident),
            pl.BlockSpec(w_out.shape, const2, **resident),
        ],
        out_specs=pl.BlockSpec((None, tile, d), lambda b, s: (b, s, 0)),
        out_shape=jax.ShapeDtypeStruct(x.shape, x.dtype),
        scratch_shapes=[pltpu.VMEM((POOL_HALO + tile, B_WIDTH), jnp.float32)],
        compiler_params=pltpu.CompilerParams(dimension_semantics=("arbitrary", "arbitrary"),
                                             vmem_limit_bytes=VMEM_LIMIT_BYTES),
        name="mixing_sublayer",
    )(x, mod, pre_g, post_g, w_in, sgu_g, w_sp, b_sp_t, w_pool, pool_scale, w_out)


def _time_strided_permutation(tile):
    n_blocks = tile // SUBLANES
    p = jnp.arange(tile)
    src = (p % SUBLANES) * n_blocks + p // SUBLANES
    return (src[:, None] == jnp.arange(tile)[None, :]).astype(jnp.bfloat16)


def _shift_rows_in(prev_blk, cur_blk):
    sub = lax.broadcasted_iota(jnp.int32, cur_blk.shape, 0)
    return jnp.where(sub == 0, pltpu.roll(prev_blk, 1, axis=0), pltpu.roll(cur_blk, 1, axis=0))


def _ffn_kernel(x_in_ref, x_res_ref, mod_ref, perm_ref, pre_g_ref, post_g_ref, w_up_ref, conv_w_ref,
                conv_b_ref, w_down_ref, o_ref, h_ref, carry_ref, f_ref, stage_ref, *, tiles_per_seq):
    g = pl.program_id(0)
    n_tiles = pl.num_programs(0) - 2
    tile, d = x_in_ref.shape
    n_blocks = tile // SUBLANES
    n_chunks = w_up_ref.shape[0]
    fc = w_up_ref.shape[2] // 2

    @pl.when(g == 0)
    def _():
        h_ref[...] = jnp.zeros(h_ref.shape, h_ref.dtype)
        f_ref[...] = jnp.zeros(f_ref.shape, f_ref.dtype)
        carry_ref[...] = jnp.zeros(carry_ref.shape, carry_ref.dtype)

    mod_exit = mod_ref[jnp.maximum(g - 2, 0) // tiles_per_seq]
    delta = _rms_scale(f_ref[...]) * (post_g_ref[...] * mod_exit[5:6])
    blocks_per_segment = n_blocks // SUBLANES
    for j in range(d // LANES):
        cols = slice(j * LANES, (j + 1) * LANES)
        stage_ref[j] = delta[:, cols]
        for q in range(n_blocks):
            start = SUBLANES * SUBLANES * (q % blocks_per_segment) + q // blocks_per_segment
            rows = slice(q * SUBLANES, (q + 1) * SUBLANES)
            o_ref[rows, cols] = (x_res_ref[rows, cols]
                                 + stage_ref[j, pl.ds(start, SUBLANES, stride=SUBLANES), :])

    def entry_stage():
        mod_in = mod_ref[jnp.minimum(g, n_tiles - 1) // tiles_per_seq]
        in_gain = pre_g_ref[...] * (1.0 + mod_in[4:5])
        h = (_rms_scale(x_in_ref[...]) * in_gain + mod_in[3:4]).astype(jnp.bfloat16)
        return jnp.dot(perm_ref[...], h, preferred_element_type=jnp.float32).astype(jnp.bfloat16)

    first_of_seq = (jnp.maximum(g - 1, 0) % tiles_per_seq) == 0

    def conv_gate(c, up):
        prev = jnp.where(first_of_seq, 0.0, carry_ref[c])
        carry_ref[c] = up[tile - 2 * SUBLANES:, :]
        back1 = _shift_rows_in(prev[SUBLANES:], up[tile - SUBLANES:])
        back2 = _shift_rows_in(prev[:SUBLANES], up[tile - 2 * SUBLANES:tile - SUBLANES])
        up1 = jnp.concatenate([back1, up[:tile - SUBLANES]], axis=0)
        up2 = jnp.concatenate([back2, back1, up[:tile - 2 * SUBLANES]], axis=0)
        cw = conv_w_ref[c]
        y = conv_b_ref[c] + up2 * cw[0:1] + up1 * cw[1:2] + up * cw[2:3]
        return (_silu(y[:, :fc]) * y[:, fc:]).astype(jnp.bfloat16)

    def up_project(c):
        return jnp.dot(h_ref[...], w_up_ref[c], preferred_element_type=jnp.float32)

    f = None
    acts = []
    up_next = up_project(0)
    for c in range(n_chunks):
        up = up_next
        if c + 1 < n_chunks:
            up_next = up_project(c + 1)
        if c + 2 == n_chunks:
            h_ref[...] = entry_stage()
        if len(acts) == DOWN_GROUP:
            lo = (c - DOWN_GROUP) * fc
            part = jnp.dot(jnp.concatenate(acts, axis=1), w_down_ref[lo:lo + DOWN_GROUP * fc, :],
                           preferred_element_type=jnp.float32)
            f = part if f is None else f + part
            acts = []
        acts.append(conv_gate(c, up))
    lo = (n_chunks - len(acts)) * fc
    part = jnp.dot(jnp.concatenate(acts, axis=1), w_down_ref[lo:, :], preferred_element_type=jnp.float32)
    f_ref[...] = part if f is None else f + part


def _channel(x, mod, pre_g, post_g, w_up_c, conv_w_c, conv_b_c, w_down):
    bsz, seq, d = x.shape
    tile = SEQ_TILE_FFN
    n_chunks, _, two_fc = w_up_c.shape
    n_tiles = bsz * seq // tile
    perm = _time_strided_permutation(tile)
    const2 = lambda g: (0, 0)
    const3 = lambda g: (0, 0, 0)
    resident = dict(pipeline_mode=pl.Buffered(1))
    out = pl.pallas_call(
        functools.partial(_ffn_kernel, tiles_per_seq=seq // tile),
        grid=(n_tiles + 2,),
        in_specs=[
            pl.BlockSpec((tile, d), lambda g: (jnp.minimum(g, n_tiles - 1), 0)),
            pl.BlockSpec((tile, d), lambda g: (jnp.maximum(g - 2, 0), 0)),
            pl.BlockSpec(mod.shape, const3, **resident),
            pl.BlockSpec(perm.shape, const2, **resident),
            pl.BlockSpec(pre_g.shape, const2, **resident),
            pl.BlockSpec(post_g.shape, const2, **resident),
            pl.BlockSpec(w_up_c.shape, const3, **resident),
            pl.BlockSpec(conv_w_c.shape, const3, **resident),
            pl.BlockSpec(conv_b_c.shape, const3, **resident),
            pl.BlockSpec(w_down.shape, const2, **resident),
        ],
        out_specs=pl.BlockSpec((tile, d), lambda g: (jnp.maximum(g - 2, 0), 0)),
        out_shape=jax.ShapeDtypeStruct((bsz * seq, d), x.dtype),
        scratch_shapes=[
            pltpu.VMEM((tile, d), jnp.bfloat16),
            pltpu.VMEM((n_chunks, 2 * SUBLANES, two_fc), jnp.float32),
            pltpu.VMEM((tile, d), jnp.float32),
            pltpu.VMEM((d // LANES, tile, LANES), jnp.float32),
        ],
        compiler_params=pltpu.CompilerParams(dimension_semantics=("arbitrary",),
                                             vmem_limit_bytes=VMEM_LIMIT_BYTES),
        name="channel_sublayer",
    )(x.reshape(bsz * seq, d), x.reshape(bsz * seq, d), mod, perm, pre_g, post_g, w_up_c, conv_w_c,
      conv_b_c, w_down)
    return out.reshape(bsz, seq, d)


def _chunk_ff_columns(a, d_ff):
    n_chunks = d_ff // FF_CHUNK
    lead = a.shape[:-1]
    gate = a[..., :d_ff].reshape(*lead, n_chunks, FF_CHUNK)
    val = a[..., d_ff:].reshape(*lead, n_chunks, FF_CHUNK)
    both = jnp.concatenate([gate, val], axis=-1)
    return jnp.moveaxis(both, -2, 0)


def kernel(x, c, w_ada, b_ada, pre_mix_g, post_mix_g, w_in, sgu_norm_g, w_spatial, b_spatial, w_pool,
           pool_scale, w_out, pre_ffn_g, post_ffn_g, w_up, conv_w, conv_b, w_down):
    depth = w_ada.shape[0]
    bsz, seq, d = x.shape
    d_ff = w_down.shape[1]
    assert seq % SEQ_TILE_MIX == 0 and seq % SEQ_TILE_FFN == 0
    assert SEQ_TILE_MIX % SUB_TILE_MIX == 0 and SUB_TILE_MIX % SGU_BLOCK == 0 and d_ff % FF_CHUNK == 0
    bf16 = jnp.bfloat16
    for l in range(depth):
        mod = _modulation(c, w_ada[l], b_ada[l])
        x = _mixing(
            x, mod, pre_mix_g[l].reshape(1, d), post_mix_g[l].reshape(1, d), w_in[l].astype(bf16),
            sgu_norm_g[l].reshape(1, A_WIDTH), w_spatial[l], b_spatial[l].T, w_pool[l].astype(bf16),
            pool_scale[l].reshape(1, B_WIDTH), w_out[l].astype(bf16))
        x = _channel(
            x, mod, pre_ffn_g[l].reshape(1, d), post_ffn_g[l].reshape(1, d),
            _chunk_ff_columns(w_up[l], d_ff).astype(bf16),
            _chunk_ff_columns(conv_w[l], d_ff),
            _chunk_ff_columns(conv_b[l].reshape(1, -1), d_ff),
            w_down[l].astype(bf16))
    return x
```
